```python
import math
import jax
import jax.numpy as jnp
from jax import lax
import numpy as np

D_MODEL = 1024
BATCH = 16
SEQ = 2048
DEPTH = 4
DEC_BATCH = 2
DEC_SEQ = 16384
PAST_LEN = 128

N_MEM = 256
EPS = 1e-6
D_S5 = D_MODEL // 2
S5_GROUP = 16
S5_GROUPS = D_S5 // S5_GROUP
S5_STATE = 64
D_SSD = D_MODEL
SSD_HEAD_DIM = 64
SSD_HEADS = D_SSD // SSD_HEAD_DIM
SSD_GROUPS = 4
SSD_HPG = SSD_HEADS // SSD_GROUPS
SSD_STATE = 128
SSD_CONV = 5
SSD_CHUNK = 128
SSD_CONV_CH = D_SSD + 2 * SSD_GROUPS * SSD_STATE
XA_HEADS = 4
XA_HEAD_DIM = D_MODEL // XA_HEADS
D_FF = 4 * D_MODEL
IN_COLS = D_S5 + D_SSD + SSD_CONV_CH + 2 * SSD_HEADS + 2 * D_MODEL
SPLIT_IDX = (D_S5, D_S5 + D_SSD, D_S5 + D_SSD + SSD_CONV_CH, D_S5 + D_SSD + SSD_CONV_CH + 2 * SSD_HEADS)

kernel_name = 'hybrid_s5_ssd_gated_encoder'


def rmsnorm(x, g):
    xf = x.astype(jnp.float32)
    y = xf * lax.rsqrt(jnp.mean(xf * xf, axis=-1, keepdims=True) + EPS)
    return (y * g.astype(jnp.float32)).astype(x.dtype)


def _cplx_combine(e1, e2):
    a1r, a1i, b1r, b1i = e1
    a2r, a2i, b2r, b2i = e2
    return (a1r * a2r - a1i * a2i,
            a1r * a2i + a1i * a2r,
            a2r * b1r - a2i * b1i + b2r,
            a2r * b1i + a2i * b1r + b2i)


def s5_scan(u, lam_re, lam_im, log_dt, b_re, b_im, c_re, c_im):
    f32 = jnp.float32
    lam_re, lam_im = lam_re.astype(f32), lam_im.astype(f32)
    b_re, b_im, c_re, c_im = b_re.astype(f32), b_im.astype(f32), c_re.astype(f32), c_im.astype(f32)
    dt = jnp.exp(log_dt.astype(f32))[:, None]
    mag = jnp.exp(lam_re * dt)
    ar, ai = mag * jnp.cos(lam_im * dt), mag * jnp.sin(lam_im * dt)
    den = lam_re * lam_re + lam_im * lam_im
    fr = ((ar - 1.0) * lam_re + ai * lam_im) / den
    fi = (ai * lam_re - (ar - 1.0) * lam_im) / den
    bbr = fr[..., None] * b_re - fi[..., None] * b_im
    bbi = fr[..., None] * b_im + fi[..., None] * b_re
    xr = jnp.einsum('blgh,gph->blgp', u, bbr)
    xi = jnp.einsum('blgh,gph->blgp', u, bbi)
    shape = (1, u.shape[1]) + ar.shape
    a_r = jnp.broadcast_to(ar, shape)
    a_i = jnp.broadcast_to(ai, shape)
    _, _, hr, hi = lax.associative_scan(_cplx_combine, (a_r, a_i, xr, xi), axis=1)
    return jnp.einsum('blgp,ghp->blgh', hr, c_re) - jnp.einsum('blgp,ghp->blgh', hi, c_im)


def s5_mixer(u, lam_re, lam_im, log_dt, b_re, b_im, c_re, c_im, d, w_glu):
    bsz, L, _ = u.shape
    uf = u.astype(jnp.float32).reshape(bsz, L, S5_GROUPS, S5_GROUP)
    y_f = s5_scan(uf, lam_re[0], lam_im[0], log_dt[0], b_re[0], b_im[0], c_re[0], c_im[0])
    y_b = jnp.flip(s5_scan(jnp.flip(uf, 1), lam_re[1], lam_im[1], log_dt[1],
                           b_re[1], b_im[1], c_re[1], c_im[1]), 1)
    y = (y_f + y_b).reshape(bsz, L, D_S5).astype(u.dtype) + d * u
    y = jax.nn.gelu(y)
    return y * jax.nn.sigmoid(y @ w_glu)


def ssd_scan(x, dt, a, bm, cm):
    bsz, L = x.shape[:2]
    q = SSD_CHUNK
    nc = L // q
    x = x.reshape(bsz, nc, q, SSD_GROUPS, SSD_HPG, SSD_HEAD_DIM)
    dt = dt.reshape(bsz, nc, q, SSD_GROUPS, SSD_HPG)
    bm = bm.reshape(bsz, nc, q, SSD_GROUPS, SSD_STATE)
    cm = cm.reshape(bsz, nc, q, SSD_GROUPS, SSD_STATE)
    xdt = x * dt[..., None]
    a_cum = jnp.cumsum(dt * a, axis=2)
    lower = jnp.tril(jnp.ones((q, q), dtype=bool))[None, None, :, :, None, None]
    seg = a_cum[:, :, :, None] - a_cum[:, :, None, :]
    decay = jnp.exp(jnp.where(lower, seg, -jnp.inf))
    cb = jnp.einsum('bctgn,bcsgn->bctsg', cm, bm)
    y_diag = jnp.einsum('bctsgj,bcsgjp->bctgjp', cb[..., None] * decay, xdt)
    decay_to_end = jnp.exp(a_cum[:, :, -1:] - a_cum)
    states = jnp.einsum('bcsgn,bcsgj,bcsgjp->bcgjpn', bm, decay_to_end, xdt)
    chunk_decay = jnp.exp(a_cum[:, :, -1])

    def step(carry, inp):
        st, dec = inp
        return carry * dec[..., None, None] + st, carry

    init = jnp.zeros_like(states[:, 0])
    _, states_in = lax.scan(step, init, (jnp.moveaxis(states, 1, 0), jnp.moveaxis(chunk_decay, 1, 0)))
    states_in = jnp.moveaxis(states_in, 0, 1)
    y_off = jnp.einsum('bctgn,bcgjpn,bctgj->bctgjp', cm, states_in, jnp.exp(a_cum))
    return (y_diag + y_off).reshape(bsz, L, SSD_GROUPS, SSD_HPG, SSD_HEAD_DIM)


def ssd_mixer(z, xbc, dt_raw, conv_w, conv_b, a_log, dt_bias, d_skip, norm_w):
    f32 = jnp.float32
    bsz, L, _ = xbc.shape
    xbc = lax.conv_general_dilated(xbc, conv_w.astype(xbc.dtype)[:, None, :], window_strides=(1,),
                                   padding=((SSD_CONV // 2, SSD_CONV // 2),),
                                   dimension_numbers=('NWC', 'WIO', 'NWC'),
                                   feature_group_count=SSD_CONV_CH)
    xbc = jax.nn.silu(xbc + conv_b)
    xs = xbc[..., :D_SSD]
    bm = xbc[..., D_SSD:D_SSD + SSD_GROUPS * SSD_STATE]
    cm = xbc[..., D_SSD + SSD_GROUPS * SSD_STATE:]
    xh = xs.astype(f32).reshape(bsz, L, SSD_GROUPS, SSD_HPG, SSD_HEAD_DIM)
    bm = bm.astype(f32).reshape(bsz, L, SSD_GROUPS, SSD_STATE)
    cm = cm.astype(f32).reshape(bsz, L, SSD_GROUPS, SSD_STATE)
    dt = jax.nn.softplus(dt_raw.astype(f32).reshape(bsz, L, 2, SSD_GROUPS, SSD_HPG)
                         + dt_bias.astype(f32).reshape(2, SSD_GROUPS, SSD_HPG))
    a = -jnp.exp(a_log.astype(f32)).reshape(2, SSD_GROUPS, SSD_HPG)
    y_f = ssd_scan(xh, dt[:, :, 0], a[0], bm, cm)
    y_b = jnp.flip(ssd_scan(jnp.flip(xh, 1), jnp.flip(dt[:, :, 1], 1), a[1],
                            jnp.flip(bm, 1), jnp.flip(cm, 1)), 1)
    y = y_f + y_b + d_skip.astype(f32).reshape(SSD_GROUPS, SSD_HPG)[..., None] * xh
    y = y.reshape(bsz, L, D_SSD).astype(z.dtype)
    return rmsnorm(y * jax.nn.silu(z), norm_w)


def memory_attention(h, mem, w_q, w_k, w_v, w_o):
    bsz, L, _ = h.shape
    m = mem.shape[1]
    q = (h @ w_q).reshape(bsz, L, XA_HEADS, XA_HEAD_DIM)
    k = (mem @ w_k).reshape(bsz, m, XA_HEADS, XA_HEAD_DIM)
    v = (mem @ w_v).reshape(bsz, m, XA_HEADS, XA_HEAD_DIM)
    s = jnp.einsum('blhd,bmhd->bhlm', q, k).astype(jnp.float32) * (XA_HEAD_DIM ** -0.5)
    p = jax.nn.softmax(s, axis=-1).astype(v.dtype)
    o = jnp.einsum('bhlm,bmhd->blhd', p, v).reshape(bsz, L, D_MODEL)
    return o @ w_o


def encoder_trunk(x, mem, p):
    for i in range(DEPTH):
        n = rmsnorm(x, p['norm_mix'][i])
        proj = n @ p['w_in'][i]
        u, z, xbc, dt_raw, gates = jnp.split(proj, SPLIT_IDX, axis=-1)
        ya = s5_mixer(u, p['s5_lam_re'][i], p['s5_lam_im'][i], p['s5_log_dt'][i],
                      p['s5_b_re'][i], p['s5_b_im'][i], p['s5_c_re'][i], p['s5_c_im'][i],
                      p['s5_d'][i], p['s5_w_glu'][i])
        yb = ssd_mixer(z, xbc, dt_raw, p['ssd_conv_w'][i], p['ssd_conv_b'][i], p['ssd_a_log'][i],
                       p['ssd_dt_bias'][i], p['ssd_d'][i], p['ssd_norm'][i])
        g_a, g_b = jnp.split(gates, 2, axis=-1)
        merged = (jax.nn.sigmoid(g_a) * (ya @ p['w_branch_a'][i])
                  + jax.nn.sigmoid(g_b) * (yb @ p['w_branch_b'][i]))
        x = x + merged @ p['w_out'][i]
        x = x + memory_attention(rmsnorm(x, p['norm_xattn'][i]), rmsnorm(mem, p['norm_mem'][i]),
                                 p['w_q'][i], p['w_k'][i], p['w_v'][i], p['w_o'][i])
        h = rmsnorm(x, p['norm_mlp'][i]) @ p['w_up'][i]
        x = x + jnp.square(jax.nn.relu(h)) @ p['w_down'][i]
    return rmsnorm(x, p['norm_final'])


def setup_inputs(seed: int = 0) -> dict:
    key = jax.random.key(seed)
    ks = iter(jax.random.split(key, 48))
    f32 = jnp.float32

    def nrm(shape, scale):
        return jax.random.normal(next(ks), shape, f32) * scale

    def gain(shape):
        return 1.0 + nrm(shape, 0.02)

    def unif(shape, lo, hi):
        return jax.random.uniform(next(ks), shape, f32, lo, hi)

    D = D_MODEL
    x_prompt = nrm((BATCH, SEQ, D), 1.0)
    x_sample = nrm((DEC_BATCH, DEC_SEQ, D), 1.0)
    mem_prompt = nrm((BATCH, N_MEM, D), 1.0)
    mem_sample = nrm((DEC_BATCH, N_MEM, D), 1.0)
    norm_mix = gain((DEPTH, D))
    w_in = nrm((DEPTH, D, IN_COLS), D ** -0.5)
    s5_lam_re = -0.5 + nrm((DEPTH, 2, S5_GROUPS, S5_STATE), 0.01)
    s5_lam_im = (math.pi * jnp.arange(S5_STATE, dtype=f32)) + nrm((DEPTH, 2, S5_GROUPS, S5_STATE), 0.01)
    s5_log_dt = unif((DEPTH, 2, S5_GROUPS), math.log(1e-3), math.log(1e-1))
    s5_b_re = nrm((DEPTH, 2, S5_GROUPS, S5_STATE, S5_GROUP), (2 * S5_GROUP) ** -0.5)
    s5_b_im = nrm((DEPTH, 2, S5_GROUPS, S5_STATE, S5_GROUP), (2 * S5_GROUP) ** -0.5)
    s5_c_re = nrm((DEPTH, 2, S5_GROUPS, S5_GROUP, S5_STATE), S5_STATE ** -0.5)
    s5_c_im = nrm((DEPTH, 2, S5_GROUPS, S5_GROUP, S5_STATE), S5_STATE ** -0.5)
    s5_d = nrm((DEPTH, D_S5), 1.0)
    s5_w_glu = nrm((DEPTH, D_S5, D_S5), D_S5 ** -0.5)
    ssd_conv_w = nrm((DEPTH, SSD_CONV, SSD_CONV_CH), SSD_CONV ** -0.5)
    ssd_conv_b = nrm((DEPTH, SSD_CONV_CH), 0.02)
    ssd_a_log = jnp.log(unif((DEPTH, 2, SSD_HEADS), 1.0, 16.0))
    dt0 = jnp.exp(unif((DEPTH, 2, SSD_HEADS), math.log(1e-3), math.log(1e-1)))
    ssd_dt_bias = dt0 + jnp.log(-jnp.expm1(-dt0))
    ssd_d = gain((DEPTH, SSD_HEADS))
    ssd_norm = gain((DEPTH, D_SSD))
    w_branch_a = nrm((DEPTH, D_S5, D), D_S5 ** -0.5)
    w_branch_b = nrm((DEPTH, D_SSD, D), D_SSD ** -0.5)
    w_out = nrm((DEPTH, D, D), D ** -0.5)
    norm_xattn = gain((DEPTH, D))
    norm_mem = gain((DEPTH, D))
    w_q = nrm((DEPTH, D, D), D ** -0.5)
    w_k = nrm((DEPTH, D, D), D ** -0.5)
    w_v = nrm((DEPTH, D, D), D ** -0.5)
    w_o = nrm((DEPTH, D, D), D ** -0.5)
    norm_mlp = gain((DEPTH, D))
    w_up = nrm((DEPTH, D, D_FF), D ** -0.5)
    w_down = nrm((DEPTH, D_FF, D), D_FF ** -0.5)
    norm_final = gain((D,))
    return {'x_prompt': x_prompt, 'x_sample': x_sample, 'mem_prompt': mem_prompt, 'mem_sample': mem_sample,
            'norm_mix': norm_mix, 'w_in': w_in,
            's5_lam_re': s5_lam_re, 's5_lam_im': s5_lam_im, 's5_log_dt': s5_log_dt,
            's5_b_re': s5_b_re, 's5_b_im': s5_b_im, 's5_c_re': s5_c_re, 's5_c_im': s5_c_im,
            's5_d': s5_d, 's5_w_glu': s5_w_glu,
            'ssd_conv_w': ssd_conv_w, 'ssd_conv_b': ssd_conv_b, 'ssd_a_log': ssd_a_log,
            'ssd_dt_bias': ssd_dt_bias, 'ssd_d': ssd_d, 'ssd_norm': ssd_norm,
            'w_branch_a': w_branch_a, 'w_branch_b': w_branch_b, 'w_out': w_out,
            'norm_xattn': norm_xattn, 'norm_mem': norm_mem, 'w_q': w_q, 'w_k': w_k, 'w_v': w_v, 'w_o': w_o,
            'norm_mlp': norm_mlp, 'w_up': w_up, 'w_down': w_down, 'norm_final': norm_final}


def reference(x_prompt, x_sample, mem_prompt, mem_sample, norm_mix, w_in,
              s5_lam_re, s5_lam_im, s5_log_dt, s5_b_re, s5_b_im, s5_c_re, s5_c_im, s5_d, s5_w_glu,
              ssd_conv_w, ssd_conv_b, ssd_a_log, ssd_dt_bias, ssd_d, ssd_norm,
              w_branch_a, w_branch_b, w_out,
              norm_xattn, norm_mem, w_q, w_k, w_v, w_o,
              norm_mlp, w_up, w_down, norm_final):
    p = dict(norm_mix=norm_mix, w_in=w_in,
             s5_lam_re=s5_lam_re, s5_lam_im=s5_lam_im, s5_log_dt=s5_log_dt,
             s5_b_re=s5_b_re, s5_b_im=s5_b_im, s5_c_re=s5_c_re, s5_c_im=s5_c_im,
             s5_d=s5_d, s5_w_glu=s5_w_glu,
             ssd_conv_w=ssd_conv_w, ssd_conv_b=ssd_conv_b, ssd_a_log=ssd_a_log,
             ssd_dt_bias=ssd_dt_bias, ssd_d=ssd_d, ssd_norm=ssd_norm,
             w_branch_a=w_branch_a, w_branch_b=w_branch_b, w_out=w_out,
             norm_xattn=norm_xattn, norm_mem=norm_mem, w_q=w_q, w_k=w_k, w_v=w_v, w_o=w_o,
             norm_mlp=norm_mlp, w_up=w_up, w_down=w_down, norm_final=norm_final)
    y_prompt = encoder_trunk(x_prompt, mem_prompt, p)
    y_sample = encoder_trunk(x_sample, mem_sample, p)
    return (y_prompt, y_sample)
```

```python
import functools
import math

import jax
import jax.numpy as jnp
from jax import lax
from jax.experimental import pallas as pl
from jax.experimental.pallas import tpu as pltpu

F32 = jnp.float32
BF16 = jnp.bfloat16

D_MODEL = 1024
EPS = 1e-6
D_S5 = 512
S5_H = 16
S5_G = D_S5 // S5_H
S5_P = 64
S5_Q = 16
S5_NS = S5_G * S5_P
D_SSD = 1024
SSD_P = 64
SSD_HEADS = D_SSD // SSD_P
SSD_G = 4
SSD_HPG = SSD_HEADS // SSD_G
SSD_N = 128
SSD_CONV = 5
SSD_CH = D_SSD + 2 * SSD_G * SSD_N
SSD_Q = 128
XA_HEADS = 4
XA_HD = D_MODEL // XA_HEADS
D_FF = 4 * D_MODEL

LANES = 128
SUBLANES = 8
VMEM_LIMIT_BYTES = 56 * 1024 * 1024

COL_XBC = 0
COL_Z = 2048
COL_GA = 3072
COL_GB = 4096
COL_DT = 5120
PROJ_COLS = 5248
W_IN_COLS = PROJ_COLS + D_S5


def _cparams(*sem):
    return pltpu.CompilerParams(dimension_semantics=sem, vmem_limit_bytes=VMEM_LIMIT_BYTES)


def _const_spec(shape):
    zeros = (0,) * len(shape)
    return pl.BlockSpec(shape, lambda *_: zeros)


def _rms(x, g):
    return x * lax.rsqrt(jnp.mean(x * x, axis=-1, keepdims=True) + EPS) * g


def _sigmoid(x):
    return 1.0 / (1.0 + jnp.exp(-x))


def _mm(a, b):
    return jnp.dot(a, b, preferred_element_type=F32)


def _split3(v):
    hi = v.astype(BF16)
    r = v - hi.astype(F32)
    mid = r.astype(BF16)
    lo = (r - mid.astype(F32)).astype(BF16)
    return hi, mid, lo


def _mm_exact_rhs(v, sel):
    hi, mid, lo = _split3(v)
    return _mm(hi, sel) + _mm(mid, sel) + _mm(lo, sel)


def _mm_exact_lhs(sel, v):
    hi, mid, lo = _split3(v)
    return _mm(sel, hi) + _mm(sel, mid) + _mm(sel, lo)


def _inproj_kernel(x_ref, g_ref, w_ref, proj_ref, u_ref):
    n = _rms(x_ref[...], g_ref[...]).astype(BF16)
    for a in range(0, PROJ_COLS, 1024):
        b = min(a + 1024, PROJ_COLS)
        proj_ref[:, a:b] = _mm(n, w_ref[:, a:b])
    u = _mm(n, w_ref[:, PROJ_COLS:W_IN_COLS])
    for k in range(D_S5 // LANES):
        u_ref[k] = u[:, k * LANES:(k + 1) * LANES]


def _inproj(x2d, g, w, tm=256):
    t = x2d.shape[0]
    return pl.pallas_call(
        _inproj_kernel,
        grid=(t // tm,),
        in_specs=[pl.BlockSpec((tm, D_MODEL), lambda i: (i, 0)),
                  _const_spec((1, D_MODEL)),
                  _const_spec((D_MODEL, W_IN_COLS))],
        out_specs=[pl.BlockSpec((tm, PROJ_COLS), lambda i: (i, 0)),
                   pl.BlockSpec((D_S5 // LANES, tm, LANES), lambda i: (0, i, 0))],
        out_shape=[jax.ShapeDtypeStruct((t, PROJ_COLS), F32),
                   jax.ShapeDtypeStruct((D_S5 // LANES, t, LANES), F32)],
        compiler_params=_cparams("parallel"),
        name="inproj",
    )(x2d, g, w)


def _s5_tables(lam_re, lam_im, log_dt, b_re, b_im, c_re, c_im):
    hp = lax.Precision.HIGHEST
    q, g, h, p = S5_Q, S5_G, S5_H, S5_P
    lam_re, lam_im = lam_re.astype(F32), lam_im.astype(F32)
    dt = jnp.exp(log_dt.astype(F32))[..., None]
    lr, th = lam_re * dt, lam_im * dt
    mag = jnp.exp(lr)
    ar, ai = mag * jnp.cos(th), mag * jnp.sin(th)
    den = lam_re * lam_re + lam_im * lam_im
    fr = ((ar - 1.0) * lam_re + ai * lam_im) / den
    fi = (ai * lam_re - (ar - 1.0) * lam_im) / den
    bbr = fr[..., None] * b_re - fi[..., None] * b_im
    bbi = fr[..., None] * b_im + fi[..., None] * b_re
    ls = jnp.arange(q + 1, dtype=F32)[:, None, None, None]
    pm = jnp.exp(ls * lr[None])
    pr, pi = pm * jnp.cos(ls * th[None]), pm * jnp.sin(ls * th[None])
    abr = pr[..., None] * bbr[None] - pi[..., None] * bbi[None]
    abi = pr[..., None] * bbi[None] + pi[..., None] * bbr[None]
    k = (jnp.einsum('dgip,ldgpj->ldgij', c_re, abr[:q], precision=hp)
         - jnp.einsum('dgip,ldgpj->ldgij', c_im, abi[:q], precision=hp))
    lag = jnp.arange(q)[None, :] - jnp.arange(q)[:, None]
    tf = jnp.where((lag >= 0)[:, :, None, None, None], k[jnp.clip(lag, 0, q - 1), 0], 0.0)
    tb = jnp.where((lag <= 0)[:, :, None, None, None], k[jnp.clip(-lag, 0, q - 1), 1], 0.0)
    tmat = (tf + tb).transpose(2, 0, 4, 1, 3).reshape(g, q * h, q * h)

    eye2 = jnp.eye(2, dtype=F32)

    def pair_ws(wr, wi):
        wr = wr.transpose(1, 0, 3, 2).reshape(g // 2, 2, q * h, p)
        wi = wi.transpose(1, 0, 3, 2).reshape(g // 2, 2, q * h, p)
        w = jnp.stack([wr, wi], axis=3)
        w = w[:, :, :, :, None, :] * eye2[None, :, None, None, :, None]
        return w.reshape(g // 2, 2 * q * h, 4 * p)

    ws = jnp.stack([pair_ws(abr[q - 1::-1, 0], abi[q - 1::-1, 0]),
                    pair_ws(abr[:q, 1], abi[:q, 1])])
    a16 = jnp.stack([pr[q].reshape(2, S5_NS), pi[q].reshape(2, S5_NS)], axis=1)

    def omat(cr, ci, prr, pii):
        crt = cr.transpose(0, 2, 1)[:, :, None, :]
        cit = ci.transpose(0, 2, 1)[:, :, None, :]
        prt = prr.transpose(1, 2, 0)[:, :, :, None]
        pit = pii.transpose(1, 2, 0)[:, :, :, None]
        o_r = crt * prt - cit * pit
        o_i = -crt * pit - cit * prt
        return o_r.reshape(g, p, q * h), o_i.reshape(g, p, q * h)

    ofr, ofi = omat(c_re[0], c_im[0], pr[1:q + 1, 0], pi[1:q + 1, 0])
    obr, obi = omat(c_re[1], c_im[1], pr[q:0:-1, 1], pi[q:0:-1, 1])
    o4 = jnp.stack([ofr, ofi, obr, obi], axis=1).reshape(g // 2, 2, 4, p, q * h)
    o4 = o4.transpose(0, 2, 1, 3, 4)
    om = o4[:, :, :, :, None, :] * eye2[None, None, :, None, :, None]
    om = om.reshape(g // 2, 8 * p, 2 * q * h)
    return tmat.astype(BF16), ws.astype(BF16), a16, om.astype(BF16)


def _s5_fold_time(u_ref, cb):
    gps = LANES // S5_H
    slabs = [[u_ref[k, pl.ds(t, cb, stride=S5_Q), :] for t in range(S5_Q)]
             for k in range(D_S5 // LANES)]
    out = []
    for g in range(S5_G):
        k, m = divmod(g, gps)
        out.append(jnp.concatenate(
            [slabs[k][t][:, S5_H * m:S5_H * (m + 1)] for t in range(S5_Q)], axis=1))
    return out


def _s5_pre_kernel(u_ref, ws_ref, a16_ref, hin_ref, s_scr, hin_scr, h_scr, *, cb):
    d = pl.program_id(0)
    i = pl.program_id(2)

    @pl.when(i == 0)
    def _():
        h_scr[...] = jnp.zeros_like(h_scr)

    ug = _s5_fold_time(u_ref, cb)
    for q in range(S5_G // 2):
        up = jnp.concatenate([ug[2 * q], ug[2 * q + 1]], axis=1).astype(BF16)
        s = _mm(up, ws_ref[0, q])
        s_scr[:, q * LANES:(q + 1) * LANES] = s[:, :LANES]
        s_scr[:, S5_NS + q * LANES:S5_NS + (q + 1) * LANES] = s[:, LANES:]
    ar = a16_ref[0, 0:1, :]
    ai = a16_ref[0, 1:2, :]

    def step(r, carry):
        hr, hi = carry
        rr = jnp.where(d == 0, r, cb - 1 - r)
        hin_scr[pl.ds(rr, 1), 0:S5_NS] = hr
        hin_scr[pl.ds(rr, 1), S5_NS:2 * S5_NS] = hi
        sr = s_scr[pl.ds(rr, 1), 0:S5_NS]
        si = s_scr[pl.ds(rr, 1), S5_NS:2 * S5_NS]
        return ar * hr - ai * hi + sr, ar * hi + ai * hr + si

    hr, hi = lax.fori_loop(0, cb, step, (h_scr[0:1, 0:S5_NS], h_scr[0:1, S5_NS:2 * S5_NS]))
    h_scr[0:1, 0:S5_NS] = hr
    h_scr[0:1, S5_NS:2 * S5_NS] = hi
    hin_ref[0, 0] = hin_scr[...].astype(BF16)


def _s5_pre(u4, ws, a16, bsz, seq, cb):
    lb = cb * S5_Q
    nblk = seq // lb
    nchunk = seq // S5_Q

    def blk(d, i):
        return jnp.where(d == 0, i, nblk - 1 - i)

    return pl.pallas_call(
        functools.partial(_s5_pre_kernel, cb=cb),
        grid=(2, bsz, nblk),
        in_specs=[pl.BlockSpec((D_S5 // LANES, lb, LANES), lambda d, b, i: (0, b * nblk + blk(d, i), 0)),
                  pl.BlockSpec((1, S5_G // 2, 2 * S5_Q * S5_H, 4 * S5_P), lambda d, b, i: (d, 0, 0, 0)),
                  pl.BlockSpec((1, 2, S5_NS), lambda d, b, i: (d, 0, 0))],
        out_specs=pl.BlockSpec((1, 1, cb, 2 * S5_NS), lambda d, b, i: (d, b, blk(d, i), 0)),
        out_shape=jax.ShapeDtypeStruct((2, bsz, nchunk, 2 * S5_NS), BF16),
        scratch_shapes=[pltpu.VMEM((cb, 2 * S5_NS), F32),
                        pltpu.VMEM((cb, 2 * S5_NS), F32),
                        pltpu.VMEM((SUBLANES, 2 * S5_NS), F32)],
        compiler_params=_cparams("arbitrary", "arbitrary", "arbitrary"),
        name="s5_pre",
    )(u4, ws, a16)


def _s5_main_kernel(u_ref, hf_ref, hb_ref, ga_ref, t_ref, o_ref, d_ref, wg_ref, wa_ref,
                    ma_ref, y_scr, *, cb):
    ug = _s5_fold_time(u_ref, cb)
    yg = []
    for q in range(S5_G // 2):
        sl_r = slice(q * LANES, (q + 1) * LANES)
        sl_i = slice(S5_NS + q * LANES, S5_NS + (q + 1) * LANES)
        hp = jnp.concatenate([hf_ref[0, 0, :, sl_r], hf_ref[0, 0, :, sl_i],
                              hb_ref[0, 0, :, sl_r], hb_ref[0, 0, :, sl_i]], axis=1)
        yh = _mm(hp, o_ref[q])
        nq = S5_Q * S5_H
        for a in range(2):
            g = 2 * q + a
            yg.append(_mm(ug[g].astype(BF16), t_ref[g]) + yh[:, a * nq:(a + 1) * nq])
    gps = LANES // S5_H
    for t in range(S5_Q):
        for k in range(D_S5 // LANES):
            z = jnp.concatenate(
                [yg[k * gps + m][:, S5_H * t:S5_H * (t + 1)] for m in range(gps)], axis=1)
            y_scr[k, pl.ds(t, cb, stride=S5_Q), :] = z
    u = jnp.concatenate([u_ref[k] for k in range(D_S5 // LANES)], axis=1)
    y = jnp.concatenate([y_scr[k] for k in range(D_S5 // LANES)], axis=1) + d_ref[...] * u
    c0 = math.sqrt(2.0 / math.pi)
    y = 0.5 * y * (1.0 + jnp.tanh(c0 * (y + 0.044715 * (y * y * y))))
    y = y * _sigmoid(_mm(y.astype(BF16), wg_ref[...]))
    ma_ref[0] = _sigmoid(ga_ref[0]) * _mm(y.astype(BF16), wa_ref[...])


def _s5_main(u4, hin, proj3, tmat, omat, s5_d, w_glu, w_a, bsz, seq, cb):
    lb = cb * S5_Q
    nblk = seq // lb
    return pl.pallas_call(
        functools.partial(_s5_main_kernel, cb=cb),
        grid=(bsz, nblk),
        in_specs=[pl.BlockSpec((D_S5 // LANES, lb, LANES), lambda b, i: (0, b * nblk + i, 0)),
                  pl.BlockSpec((1, 1, cb, 2 * S5_NS), lambda b, i: (0, b, i, 0)),
                  pl.BlockSpec((1, 1, cb, 2 * S5_NS), lambda b, i: (1, b, i, 0)),
                  pl.BlockSpec((1, lb, D_MODEL), lambda b, i: (b, i, COL_GA // D_MODEL)),
                  _const_spec(tmat.shape), _const_spec(omat.shape), _const_spec((1, D_S5)),
                  _const_spec((D_S5, D_S5)), _const_spec((D_S5, D_MODEL))],
        out_specs=pl.BlockSpec((1, lb, D_MODEL), lambda b, i: (b, i, 0)),
        out_shape=jax.ShapeDtypeStruct((bsz, seq, D_MODEL), F32),
        scratch_shapes=[pltpu.VMEM((D_S5 // LANES, lb, LANES), F32)],
        compiler_params=_cparams("parallel", "parallel"),
        name="s5_main",
    )(u4, hin, hin, proj3, tmat, omat, s5_d, w_glu, w_a)


def _softplus(x):
    return jnp.maximum(x, 0.0) + jnp.log(1.0 + jnp.exp(-jnp.abs(x)))


def _ssd_conv(xbc_ref, xp_ref, xn_ref, cw_ref, cbias_ref, xw_scr, first, last, tb, cols):
    h = SUBLANES
    xw_scr[0:h, :] = jnp.where(first, 0.0, xp_ref[0])
    xw_scr[h:h + tb, :] = xbc_ref[0]
    xw_scr[h + tb:2 * h + tb, :] = jnp.where(last, 0.0, xn_ref[0])
    acc = cbias_ref[:, 0:cols]
    for k in range(SSD_CONV):
        off = h - SSD_CONV // 2 + k
        acc = acc + cw_ref[k:k + 1, 0:cols] * xw_scr[off:off + tb, 0:cols]
    return acc * _sigmoid(acc)


def _ssd_decays(dt_raw, dtb_ref, alog_ref):
    dtv = _softplus(dt_raw + dtb_ref[...])
    dta = dtv * (-jnp.exp(alog_ref[...]))
    row = lax.broadcasted_iota(jnp.int32, (SSD_Q, SSD_Q), 0)
    col = lax.broadcasted_iota(jnp.int32, (SSD_Q, SSD_Q), 1)
    tri = jnp.where(row >= col, 1.0, 0.0).astype(BF16)
    inc = _mm_exact_lhs(tri, dta)
    return dtv, dta, inc, row, col


def _ssd_pre_kernel(xbc_ref, xp_ref, xn_ref, dt_ref, cw_ref, cbias_ref, dtb_ref, alog_ref, eb_ref,
                    hb_ref, h_scr, xw_scr, cv_scr, *, nch, nblk):
    i = pl.program_id(1)
    tb = nch * SSD_Q
    ncol = D_SSD + SSD_G * SSD_N

    @pl.when(i == 0)
    def _():
        h_scr[...] = jnp.zeros_like(h_scr)

    cv_scr[...] = _ssd_conv(xbc_ref, xp_ref, xn_ref, cw_ref, cbias_ref, xw_scr,
                            i == nblk - 1, i == 0, tb, ncol)

    def chunk(k, carry):
        ci = nch - 1 - k
        r0 = pl.multiple_of(ci * SSD_Q, SSD_Q)
        xs = cv_scr[pl.ds(r0, SSD_Q), 0:D_SSD]
        bm = cv_scr[pl.ds(r0, SSD_Q), D_SSD:ncol]
        dtv, dta, inc, _, _ = _ssd_decays(dt_ref[0, pl.ds(r0, SSD_Q), :], dtb_ref, alog_ref)
        exc = inc - dta
        wb = _mm_exact_rhs(jnp.exp(exc) * dtv, eb_ref[...])
        decb = _mm_exact_rhs(jnp.exp(inc), eb_ref[...])[SSD_Q - 1:SSD_Q, :]
        xw = (xs * wb).astype(BF16)
        gw = SSD_HPG * SSD_P
        for g in range(SSD_G):
            hg = h_scr[g]
            hb_ref[0, ci, g] = hg.astype(BF16)
            bg = bm[:, g * SSD_N:(g + 1) * SSD_N].astype(BF16)
            sg = lax.dot_general(bg, xw[:, g * gw:(g + 1) * gw], (((0,), (0,)), ((), ())),
                                 preferred_element_type=F32)
            h_scr[g] = hg * decb[:, g * gw:(g + 1) * gw] + sg
        return carry

    lax.fori_loop(0, nch, chunk, 0)


def _ssd_halo_specs(tb, seq, rev_nblk=None):
    r = tb // SUBLANES
    last = seq // SUBLANES - 1

    def bi(i):
        return i if rev_nblk is None else rev_nblk - 1 - i

    return [pl.BlockSpec((1, tb, SSD_CH), lambda b, i: (b, bi(i), 0)),
            pl.BlockSpec((1, SUBLANES, SSD_CH), lambda b, i: (b, jnp.maximum(bi(i) * r - 1, 0), 0)),
            pl.BlockSpec((1, SUBLANES, SSD_CH), lambda b, i: (b, jnp.minimum((bi(i) + 1) * r, last), 0))]


def _ssd_pre(proj3, cw, cbias, dtb, alog, eb, bsz, seq, nch):
    tb = nch * SSD_Q
    nblk = seq // tb
    gw = SSD_HPG * SSD_P
    return pl.pallas_call(
        functools.partial(_ssd_pre_kernel, nch=nch, nblk=nblk),
        grid=(bsz, nblk),
        in_specs=_ssd_halo_specs(tb, seq, nblk) + [
            pl.BlockSpec((1, tb, LANES), lambda b, i: (b, nblk - 1 - i, COL_DT // LANES)),
            _const_spec(cw.shape), _const_spec(cbias.shape), _const_spec(dtb.shape),
            _const_spec(alog.shape), _const_spec(eb.shape)],
        out_specs=pl.BlockSpec((1, nch, SSD_G, SSD_N, gw), lambda b, i: (b, nblk - 1 - i, 0, 0, 0)),
        out_shape=jax.ShapeDtypeStruct((bsz, seq // SSD_Q, SSD_G, SSD_N, gw), BF16),
        scratch_shapes=[pltpu.VMEM((SSD_G, SSD_N, gw), F32),
                        pltpu.VMEM((tb + 2 * SUBLANES, SSD_CH), F32),
                        pltpu.VMEM((tb, D_SSD + SSD_G * SSD_N), F32)],
        compiler_params=_cparams("arbitrary", "arbitrary"),
        name="ssd_pre",
    )(proj3, proj3, proj3, proj3, cw, cbias, dtb, alog, eb)


def _ssd_main_kernel(xbc_ref, xp_ref, xn_ref, dt_ref, z_ref, gb_ref, x_ref, ma_ref, hb_ref,
                     cw_ref, cbias_ref, dtb_ref, alog_ref, dfull_ref, nw_ref, ef_ref, eb_ref,
                     wb_ref, wo_ref, out_ref, hf_scr, xw_scr, cv_scr, y_scr, *, nch, nblk):
    i = pl.program_id(1)
    tb = nch * SSD_Q
    gw = SSD_HPG * SSD_P

    @pl.when(i == 0)
    def _():
        hf_scr[...] = jnp.zeros_like(hf_scr)

    cv_scr[...] = _ssd_conv(xbc_ref, xp_ref, xn_ref, cw_ref, cbias_ref, xw_scr,
                            i == 0, i == nblk - 1, tb, SSD_CH)

    def chunk(ci, carry):
        r0 = pl.multiple_of(ci * SSD_Q, SSD_Q)
        xs = cv_scr[pl.ds(r0, SSD_Q), 0:D_SSD]
        bm = cv_scr[pl.ds(r0, SSD_Q), D_SSD:D_SSD + SSD_G * SSD_N]
        cm = cv_scr[pl.ds(r0, SSD_Q), D_SSD + SSD_G * SSD_N:SSD_CH]
        dtv, dta, inc, row, col = _ssd_decays(dt_ref[0, pl.ds(r0, SSD_Q), :], dtb_ref, alog_ref)
        exc = inc - dta
        tot = inc[SSD_Q - 1:SSD_Q, :]
        inc_t, exc_t, dt_t = inc.T, exc.T, dtv.T
        ef_full = _mm_exact_rhs(jnp.exp(inc), ef_ref[...])
        wf_full = _mm_exact_rhs(jnp.exp(tot - inc) * dtv, ef_ref[...])
        eb_full = _mm_exact_rhs(jnp.exp(tot - exc), eb_ref[...])
        decf = ef_full[SSD_Q - 1:SSD_Q, :]
        xb = xs.astype(BF16)
        xwf = (xs * wf_full).astype(BF16)
        causal = row >= col
        anti = row <= col
        for g in range(SSD_G):
            bg = bm[:, g * SSD_N:(g + 1) * SSD_N].astype(BF16)
            cg = cm[:, g * SSD_N:(g + 1) * SSD_N].astype(BF16)
            cbm = lax.dot_general(cg, bg, (((1,), (1,)), ((), ())), preferred_element_type=F32)
            hfg = hf_scr[g]
            yoff = (ef_full[:, g * gw:(g + 1) * gw] * _mm(cg, hfg.astype(BF16))
                    + eb_full[:, g * gw:(g + 1) * gw] * _mm(cg, hb_ref[0, ci, g]))
            for jj in range(SSD_HPG):
                j = g * SSD_HPG + jj
                jb = SSD_HEADS + j
                segf = inc[:, j:j + 1] - inc_t[j:j + 1, :]
                lf = jnp.where(causal, jnp.exp(jnp.minimum(segf, 0.0)), 0.0) * dt_t[j:j + 1, :]
                segb = exc_t[jb:jb + 1, :] - exc[:, jb:jb + 1]
                lb = jnp.where(anti, jnp.exp(jnp.minimum(segb, 0.0)), 0.0) * dt_t[jb:jb + 1, :]
                m = (cbm * (lf + lb)).astype(BF16)
                yd = _mm(m, xb[:, j * SSD_P:(j + 1) * SSD_P])
                y_scr[pl.ds(r0, SSD_Q), j * SSD_P:(j + 1) * SSD_P] = (
                    yd + yoff[:, jj * SSD_P:(jj + 1) * SSD_P])
            sg = lax.dot_general(bg, xwf[:, g * gw:(g + 1) * gw], (((0,), (0,)), ((), ())),
                                 preferred_element_type=F32)
            hf_scr[g] = hfg * decf[:, g * gw:(g + 1) * gw] + sg
        return carry

    lax.fori_loop(0, nch, chunk, 0)

    y = y_scr[...] + dfull_ref[...] * cv_scr[:, 0:D_SSD]
    z = z_ref[0]
    y = _rms(y * (z * _sigmoid(z)), nw_ref[...])
    merged = ma_ref[0] + _sigmoid(gb_ref[0]) * _mm(y.astype(BF16), wb_ref[...])
    out_ref[0] = x_ref[0] + _mm(merged.astype(BF16), wo_ref[...])


def _ssd_main(proj3, x, ma, hb, cw, cbias, dtb, alog, dfull, nw, ef, eb, w_b, w_o, bsz, seq, nch):
    tb = nch * SSD_Q
    nblk = seq // tb
    gw = SSD_HPG * SSD_P
    tok = pl.BlockSpec((1, tb, D_MODEL), lambda b, i: (b, i, 0))
    return pl.pallas_call(
        functools.partial(_ssd_main_kernel, nch=nch, nblk=nblk),
        grid=(bsz, nblk),
        in_specs=_ssd_halo_specs(tb, seq) + [
            pl.BlockSpec((1, tb, LANES), lambda b, i: (b, i, COL_DT // LANES)),
            pl.BlockSpec((1, tb, D_SSD), lambda b, i: (b, i, COL_Z // D_SSD)),
            pl.BlockSpec((1, tb, D_MODEL), lambda b, i: (b, i, COL_GB // D_MODEL)),
            tok, tok,
            pl.BlockSpec((1, nch, SSD_G, SSD_N, gw), lambda b, i: (b, i, 0, 0, 0)),
            _const_spec(cw.shape), _const_spec(cbias.shape), _const_spec(dtb.shape),
            _const_spec(alog.shape), _const_spec(dfull.shape), _const_spec(nw.shape),
            _const_spec(ef.shape), _const_spec(eb.shape), _const_spec(w_b.shape),
            _const_spec(w_o.shape)],
        out_specs=tok,
        out_shape=jax.ShapeDtypeStruct((bsz, seq, D_MODEL), F32),
        scratch_shapes=[pltpu.VMEM((SSD_G, SSD_N, gw), F32),
                        pltpu.VMEM((tb + 2 * SUBLANES, SSD_CH), F32),
                        pltpu.VMEM((tb, SSD_CH), F32),
                        pltpu.VMEM((tb, D_SSD), F32)],
        compiler_params=_cparams("arbitrary", "arbitrary"),
        name="ssd_main",
    )(proj3, proj3, proj3, proj3, proj3, proj3, x, ma, hb, cw, cbias, dtb, alog, dfull, nw,
      ef, eb, w_b, w_o)


def _memkv_kernel(m_ref, g_ref, wk_ref, wv_ref, k_ref, v_ref):
    n = _rms(m_ref[0], g_ref[...]).astype(BF16)
    k_ref[0] = _mm(n, wk_ref[...]).astype(BF16)
    v_ref[0] = _mm(n, wv_ref[...]).astype(BF16)


def _memkv(mem, g, wk, wv):
    bsz, m, _ = mem.shape
    spec = pl.BlockSpec((1, m, D_MODEL), lambda b: (b, 0, 0))
    return pl.pallas_call(
        _memkv_kernel,
        grid=(bsz,),
        in_specs=[spec, _const_spec((1, D_MODEL)), _const_spec(wk.shape), _const_spec(wv.shape)],
        out_specs=[spec, spec],
        out_shape=[jax.ShapeDtypeStruct(mem.shape, BF16)] * 2,
        compiler_params=_cparams("parallel"),
        name="memkv",
    )(mem, g, wk, wv)


def _xattn_kernel(x_ref, k_ref, v_ref, g_ref, wq_ref, wo_ref, o_ref):
    x = x_ref[0]
    q = _mm(_rms(x, g_ref[...]).astype(BF16), wq_ref[...])
    scale = XA_HD ** -0.5
    heads = []
    for h in range(XA_HEADS):
        sl = slice(h * XA_HD, (h + 1) * XA_HD)
        s = lax.dot_general(q[:, sl].astype(BF16), k_ref[0, :, sl], (((1,), (1,)), ((), ())),
                            preferred_element_type=F32) * scale
        p = jnp.exp(s - jnp.max(s, axis=-1, keepdims=True))
        p = p / jnp.sum(p, axis=-1, keepdims=True)
        heads.append(_mm(p.astype(BF16), v_ref[0, :, sl]))
    o = jnp.concatenate(heads, axis=1).astype(BF16)
    o_ref[0] = x + _mm(o, wo_ref[...])


def _xattn(x, k, v, g, wq, wo, tm=512):
    bsz, seq, _ = x.shape
    m = k.shape[1]
    tok = pl.BlockSpec((1, tm, D_MODEL), lambda b, i: (b, i, 0))
    kv = pl.BlockSpec((1, m, D_MODEL), lambda b, i: (b, 0, 0))
    return pl.pallas_call(
        _xattn_kernel,
        grid=(bsz, seq // tm),
        in_specs=[tok, kv, kv, _const_spec((1, D_MODEL)), _const_spec(wq.shape), _const_spec(wo.shape)],
        out_specs=tok,
        out_shape=jax.ShapeDtypeStruct(x.shape, F32),
        compiler_params=_cparams("parallel", "parallel"),
        name="xattn",
    )(x, k, v, g, wq, wo)


def _mlp_kernel(x_ref, g_ref, wu_ref, wd_ref, gf_ref, o_ref, *, final):
    x = x_ref[...]
    n = _rms(x, g_ref[...]).astype(BF16)
    acc = x
    fc = D_MODEL
    for c in range(D_FF // fc):
        h = jnp.maximum(_mm(n, wu_ref[:, c * fc:(c + 1) * fc]), 0.0)
        acc = acc + _mm((h * h).astype(BF16), wd_ref[c * fc:(c + 1) * fc, :])
    o_ref[...] = _rms(acc, gf_ref[...]) if final else acc


def _mlp(x2d, g, wu, wd, gf, final, tm=512):
    t = x2d.shape[0]
    tok = pl.BlockSpec((tm, D_MODEL), lambda i: (i, 0))
    return pl.pallas_call(
        functools.partial(_mlp_kernel, final=final),
        grid=(t // tm,),
        in_specs=[tok, _const_spec((1, D_MODEL)), _const_spec(wu.shape), _const_spec(wd.shape),
                  _const_spec((1, D_MODEL))],
        out_specs=tok,
        out_shape=jax.ShapeDtypeStruct(x2d.shape, F32),
        compiler_params=_cparams("parallel"),
        name="mlp_final" if final else "mlp",
    )(x2d, g, wu, wd, gf)


def _row(v, width=None):
    v = v.astype(F32).reshape(1, -1)
    if width is not None and v.shape[1] < width:
        v = jnp.pad(v, ((0, 0), (0, width - v.shape[1])))
    return v


def _head_selectors():
    lane = jnp.arange(LANES)[:, None]
    head = (jnp.arange(D_SSD) // SSD_P)[None, :]
    return (lane == head).astype(BF16), (lane == head + SSD_HEADS).astype(BF16)


def _pack_layer(p, i):
    w_in = p['w_in'][i]
    o_u, o_z, o_x, o_dt = D_S5, D_S5 + D_SSD, D_S5 + D_SSD + SSD_CH, D_S5 + D_SSD + SSD_CH + 2 * SSD_HEADS
    w_dt = jnp.pad(w_in[:, o_x:o_dt], ((0, 0), (0, LANES - 2 * SSD_HEADS)))
    w_packed = jnp.concatenate(
        [w_in[:, o_z:o_x], w_in[:, o_u:o_z], w_in[:, o_dt:o_dt + D_MODEL], w_in[:, o_dt + D_MODEL:],
         w_dt, w_in[:, :o_u]], axis=1).astype(BF16)
    tmat, ws, a16, omat = _s5_tables(p['s5_lam_re'][i], p['s5_lam_im'][i], p['s5_log_dt'][i],
                                     p['s5_b_re'][i], p['s5_b_im'][i], p['s5_c_re'][i], p['s5_c_im'][i])
    cw = jnp.pad(p['ssd_conv_w'][i].astype(F32), ((0, SUBLANES - SSD_CONV), (0, 0)))
    return dict(
        g_mix=_row(p['norm_mix'][i]), w_in=w_packed,
        tmat=tmat, ws=ws, a16=a16, omat=omat, s5_d=_row(p['s5_d'][i]),
        w_glu=p['s5_w_glu'][i].astype(BF16), w_a=p['w_branch_a'][i].astype(BF16),
        cw=cw, cbias=_row(p['ssd_conv_b'][i]), dtb=_row(p['ssd_dt_bias'][i], LANES),
        alog=_row(p['ssd_a_log'][i], LANES), dfull=_row(jnp.repeat(p['ssd_d'][i], SSD_P)),
        nw=_row(p['ssd_norm'][i]), w_b=p['w_branch_b'][i].astype(BF16), w_o=p['w_out'][i].astype(BF16),
        g_xa=_row(p['norm_xattn'][i]), g_mem=_row(p['norm_mem'][i]),
        wq=p['w_q'][i].astype(BF16), wk=p['w_k'][i].astype(BF16), wv=p['w_v'][i].astype(BF16),
        wo=p['w_o'][i].astype(BF16),
        g_mlp=_row(p['norm_mlp'][i]), wu=p['w_up'][i].astype(BF16), wd=p['w_down'][i].astype(BF16))


def _trunk(x, mem, layers, g_final, s5_cb, ssd_nch):
    bsz, seq, _ = x.shape
    t = bsz * seq
    ef, eb = _head_selectors()
    for li, lp in enumerate(layers):
        proj, u4 = _inproj(x.reshape(t, D_MODEL), lp['g_mix'], lp['w_in'])
        proj3 = proj.reshape(bsz, seq, PROJ_COLS)
        hin = _s5_pre(u4, lp['ws'], lp['a16'], bsz, seq, s5_cb)
        ma = _s5_main(u4, hin, proj3, lp['tmat'], lp['omat'], lp['s5_d'], lp['w_glu'], lp['w_a'],
                      bsz, seq, s5_cb)
        hb = _ssd_pre(proj3, lp['cw'], lp['cbias'], lp['dtb'], lp['alog'], eb, bsz, seq, ssd_nch)
        x = _ssd_main(proj3, x, ma, hb, lp['cw'], lp['cbias'], lp['dtb'], lp['alog'], lp['dfull'],
                      lp['nw'], ef, eb, lp['w_b'], lp['w_o'], bsz, seq, ssd_nch)
        k, v = _memkv(mem, lp['g_mem'], lp['wk'], lp['wv'])
        x = _xattn(x, k, v, lp['g_xa'], lp['wq'], lp['wo'])
        x = _mlp(x.reshape(t, D_MODEL), lp['g_mlp'], lp['wu'], lp['wd'], g_final,
                 final=li == len(layers) - 1).reshape(bsz, seq, D_MODEL)
    return x


def kernel(x_prompt, x_sample, mem_prompt, mem_sample, norm_mix, w_in, s5_lam_re, s5_lam_im, s5_log_dt, s5_b_re, s5_b_im, s5_c_re, s5_c_im, s5_d, s5_w_glu, ssd_conv_w, ssd_conv_b, ssd_a_log, ssd_dt_bias, ssd_d, ssd_norm, w_branch_a, w_branch_b, w_out, norm_xattn, norm_mem, w_q, w_k, w_v, w_o, norm_mlp, w_up, w_down, norm_final):
    p = dict(norm_mix=norm_mix, w_in=w_in,
             s5_lam_re=s5_lam_re, s5_lam_im=s5_lam_im, s5_log_dt=s5_log_dt,
             s5_b_re=s5_b_re, s5_b_im=s5_b_im, s5_c_re=s5_c_re, s5_c_im=s5_c_im,
             s5_d=s5_d, s5_w_glu=s5_w_glu,
             ssd_conv_w=ssd_conv_w, ssd_conv_b=ssd_conv_b, ssd_a_log=ssd_a_log,
             ssd_dt_bias=ssd_dt_bias, ssd_d=ssd_d, ssd_norm=ssd_norm,
             w_branch_a=w_branch_a, w_branch_b=w_branch_b, w_out=w_out,
             norm_xattn=norm_xattn, norm_mem=norm_mem, w_q=w_q, w_k=w_k, w_v=w_v, w_o=w_o,
             norm_mlp=norm_mlp, w_up=w_up, w_down=w_down)
    layers = [_pack_layer(p, i) for i in range(norm_mix.shape[0])]
    g_final = _row(norm_final)
    y_prompt = _trunk(x_prompt, mem_prompt, layers, g_final, s5_cb=64, ssd_nch=2)
    y_sample = _trunk(x_sample, mem_sample, layers, g_final, s5_cb=64, ssd_nch=2)
    return (y_prompt, y_sample)
```

```python
import functools
import math

import jax
import jax.numpy as jnp
from jax import lax
from jax.experimental import pallas as pl
from jax.experimental.pallas import tpu as pltpu

F32 = jnp.float32
BF16 = jnp.bfloat16

D_MODEL = 1024
EPS = 1e-6
D_S5 = 512
S5_H = 16
S5_G = D_S5 // S5_H
S5_P = 64
S5_Q = 16
S5_NS = S5_G * S5_P
D_SSD = 1024
SSD_P = 64
SSD_HEADS = D_SSD // SSD_P
SSD_G = 4
SSD_HPG = SSD_HEADS // SSD_G
SSD_N = 128
SSD_CONV = 5
SSD_CH = D_SSD + 2 * SSD_G * SSD_N
SSD_Q = 128
XA_HEADS = 4
XA_HD = D_MODEL // XA_HEADS
D_FF = 4 * D_MODEL

LANES = 128
SUBLANES = 8
VMEM_LIMIT_BYTES = 56 * 1024 * 1024

COL_Z = 0
COL_GA = 1024
COL_GB = 2048
COL_DT = 3072
PROJ_COLS = 3200
W_COL_XBC = PROJ_COLS
W_COL_U = W_COL_XBC + SSD_CH
W_IN_COLS = W_COL_U + D_S5
CONV_HALO = 16


def _cparams(*sem):
    return pltpu.CompilerParams(dimension_semantics=sem, vmem_limit_bytes=VMEM_LIMIT_BYTES)


def _const_spec(shape):
    zeros = (0,) * len(shape)
    return pl.BlockSpec(shape, lambda *_: zeros)


def _rms(x, g):
    return x * lax.rsqrt(jnp.mean(x * x, axis=-1, keepdims=True) + EPS) * g


def _sigmoid(x):
    return 1.0 / (1.0 + jnp.exp(-x))


def _silu(x):
    return x * _sigmoid(x)


def _mm(a, b):
    return jnp.dot(a, b, preferred_element_type=F32)


def _split3(v):
    hi = v.astype(BF16)
    r = v - hi.astype(F32)
    mid = r.astype(BF16)
    lo = (r - mid.astype(F32)).astype(BF16)
    return hi, mid, lo


def _mm_exact_rhs(v, sel):
    hi, mid, lo = _split3(v)
    return _mm(hi, sel) + _mm(mid, sel) + _mm(lo, sel)


def _inproj_kernel(x_ref, xp_ref, xn_ref, g_ref, w_ref, cw_ref, cbias_ref, proj_ref, cv_ref, u_ref,
                   ne_scr, xbc_scr, *, tm, tiles_per_seq):
    i = pl.program_id(0)
    h = CONV_HALO
    rows = tm + 2 * h
    ne_scr[0:h] = _rms(xp_ref[...], g_ref[...]).astype(BF16)
    ne_scr[h:h + tm] = _rms(x_ref[...], g_ref[...]).astype(BF16)
    ne_scr[h + tm:rows] = _rms(xn_ref[...], g_ref[...]).astype(BF16)
    pos = lax.rem(i, tiles_per_seq)
    cc = 512
    for ci, c0 in enumerate(range(0, SSD_CH, cc)):
        xbc = _mm(ne_scr[...], w_ref[:, W_COL_XBC + c0:W_COL_XBC + c0 + cc])
        n = ne_scr[h:h + tm]
        if ci < PROJ_COLS // 1024:
            proj_ref[:, ci * 1024:(ci + 1) * 1024] = _mm(n, w_ref[:, ci * 1024:(ci + 1) * 1024])
        else:
            proj_ref[:, COL_DT:PROJ_COLS] = _mm(n, w_ref[:, COL_DT:PROJ_COLS])
            u = _mm(n, w_ref[:, W_COL_U:W_IN_COLS])
            for k in range(D_S5 // LANES):
                u_ref[k] = u[:, k * LANES:(k + 1) * LANES]
        xbc_scr[ci, 0:h] = jnp.where(pos > 0, xbc[0:h], 0.0)
        xbc_scr[ci, h:h + tm] = xbc[h:h + tm]
        xbc_scr[ci, h + tm:rows] = jnp.where(pos < tiles_per_seq - 1, xbc[h + tm:rows], 0.0)
        rb, lb, pad = 64, 256, SUBLANES
        win = rb + 2 * pad
        for r0 in range(0, tm, rb):
            for l0 in range(0, cc, lb):
                xw = xbc_scr[ci, h + r0 - pad:h + r0 - pad + win, l0:l0 + lb]
                acc = cbias_ref[:, c0 + l0:c0 + l0 + lb]
                for k in range(SSD_CONV):
                    s = k - SSD_CONV // 2
                    xs = xw if s == 0 else pltpu.roll(xw, (win - s) % win, 0)
                    acc = acc + cw_ref[k:k + 1, c0 + l0:c0 + l0 + lb] * xs[pad:pad + rb]
                cv_ref[r0:r0 + rb, c0 + l0:c0 + l0 + lb] = _silu(acc).astype(BF16)


def _inproj(x2d, g, w, cw, cbias, seq, tm=256):
    t = x2d.shape[0]
    hb = tm // CONV_HALO
    last = t // CONV_HALO - 1
    return pl.pallas_call(
        functools.partial(_inproj_kernel, tm=tm, tiles_per_seq=seq // tm),
        grid=(t // tm,),
        in_specs=[pl.BlockSpec((tm, D_MODEL), lambda i: (i, 0)),
                  pl.BlockSpec((CONV_HALO, D_MODEL), lambda i: (jnp.maximum(i * hb - 1, 0), 0)),
                  pl.BlockSpec((CONV_HALO, D_MODEL), lambda i: (jnp.minimum((i + 1) * hb, last), 0)),
                  _const_spec((1, D_MODEL)),
                  _const_spec((D_MODEL, W_IN_COLS)),
                  _const_spec(cw.shape), _const_spec(cbias.shape)],
        out_specs=[pl.BlockSpec((tm, PROJ_COLS), lambda i: (i, 0)),
                   pl.BlockSpec((tm, SSD_CH), lambda i: (i, 0)),
                   pl.BlockSpec((D_S5 // LANES, tm, LANES), lambda i: (0, i, 0))],
        out_shape=[jax.ShapeDtypeStruct((t, PROJ_COLS), F32),
                   jax.ShapeDtypeStruct((t, SSD_CH), BF16),
                   jax.ShapeDtypeStruct((D_S5 // LANES, t, LANES), F32)],
        scratch_shapes=[pltpu.VMEM((tm + 2 * CONV_HALO, D_MODEL), BF16),
                        pltpu.VMEM((SSD_CH // 512, tm + 2 * CONV_HALO, 512), F32)],
        compiler_params=_cparams("parallel"),
        name="inproj",
    )(x2d, x2d, x2d, g, w, cw, cbias)


def _s5_tables(lam_re, lam_im, log_dt, b_re, b_im, c_re, c_im):
    hp = lax.Precision.HIGHEST
    q, g, h, p = S5_Q, S5_G, S5_H, S5_P
    lam_re, lam_im = lam_re.astype(F32), lam_im.astype(F32)
    dt = jnp.exp(log_dt.astype(F32))[..., None]
    lr, th = lam_re * dt, lam_im * dt
    mag = jnp.exp(lr)
    ar, ai = mag * jnp.cos(th), mag * jnp.sin(th)
    den = lam_re * lam_re + lam_im * lam_im
    fr = ((ar - 1.0) * lam_re + ai * lam_im) / den
    fi = (ai * lam_re - (ar - 1.0) * lam_im) / den
    bbr = fr[..., None] * b_re - fi[..., None] * b_im
    bbi = fr[..., None] * b_im + fi[..., None] * b_re
    ls = jnp.arange(q + 1, dtype=F32)[:, None, None, None]
    pm = jnp.exp(ls * lr[None])
    pr, pi = pm * jnp.cos(ls * th[None]), pm * jnp.sin(ls * th[None])
    abr = pr[..., None] * bbr[None] - pi[..., None] * bbi[None]
    abi = pr[..., None] * bbi[None] + pi[..., None] * bbr[None]
    k = (jnp.einsum('dgip,ldgpj->ldgij', c_re, abr[:q], precision=hp)
         - jnp.einsum('dgip,ldgpj->ldgij', c_im, abi[:q], precision=hp))
    lag = jnp.arange(q)[None, :] - jnp.arange(q)[:, None]
    tf = jnp.where((lag >= 0)[:, :, None, None, None], k[jnp.clip(lag, 0, q - 1), 0], 0.0)
    tb = jnp.where((lag <= 0)[:, :, None, None, None], k[jnp.clip(-lag, 0, q - 1), 1], 0.0)
    tmat = (tf + tb).transpose(2, 0, 4, 1, 3).reshape(g, q * h, q * h)

    eye2 = jnp.eye(2, dtype=F32)

    def pair_ws(wr, wi):
        wr = wr.transpose(1, 0, 3, 2).reshape(g // 2, 2, q * h, p)
        wi = wi.transpose(1, 0, 3, 2).reshape(g // 2, 2, q * h, p)
        w = jnp.stack([wr, wi], axis=3)
        w = w[:, :, :, :, None, :] * eye2[None, :, None, None, :, None]
        return w.reshape(g // 2, 2 * q * h, 4 * p)

    ws = jnp.concatenate([pair_ws(abr[q - 1::-1, 0], abi[q - 1::-1, 0]),
                          pair_ws(abr[:q, 1], abi[:q, 1])], axis=-1)
    a16 = jnp.stack([pr[q].reshape(2, S5_NS), pi[q].reshape(2, S5_NS)], axis=1)

    def omat(cr, ci, prr, pii):
        crt = cr.transpose(0, 2, 1)[:, :, None, :]
        cit = ci.transpose(0, 2, 1)[:, :, None, :]
        prt = prr.transpose(1, 2, 0)[:, :, :, None]
        pit = pii.transpose(1, 2, 0)[:, :, :, None]
        o_r = crt * prt - cit * pit
        o_i = -crt * pit - cit * prt
        return o_r.reshape(g, p, q * h), o_i.reshape(g, p, q * h)

    ofr, ofi = omat(c_re[0], c_im[0], pr[1:q + 1, 0], pi[1:q + 1, 0])
    obr, obi = omat(c_re[1], c_im[1], pr[q:0:-1, 1], pi[q:0:-1, 1])
    o4 = jnp.stack([ofr, ofi, obr, obi], axis=1).reshape(g // 2, 2, 4, p, q * h)
    o4 = o4.transpose(0, 2, 1, 3, 4)
    om = o4[:, :, :, :, None, :] * eye2[None, None, :, None, :, None]
    om = om.reshape(g // 2, 8 * p, 2 * q * h)
    return tmat.astype(BF16), ws.astype(BF16), a16, om.astype(BF16)


def _s5_fold_time(u_ref, cb):
    gps = LANES // S5_H
    slabs = [[u_ref[k, pl.ds(t, cb, stride=S5_Q), :] for t in range(S5_Q)]
             for k in range(D_S5 // LANES)]
    out = []
    for g in range(S5_G):
        k, m = divmod(g, gps)
        out.append(jnp.concatenate(
            [slabs[k][t][:, S5_H * m:S5_H * (m + 1)] for t in range(S5_Q)], axis=1))
    return out


def _s5_states_kernel(u_ref, ws_ref, uf_ref, s_ref, *, cb):
    ug = _s5_fold_time(u_ref, cb)
    pw = 2 * S5_Q * S5_H
    for q in range(S5_G // 2):
        up = jnp.concatenate([ug[2 * q], ug[2 * q + 1]], axis=1).astype(BF16)
        uf_ref[0, :, q * pw:(q + 1) * pw] = up
        s = _mm(up, ws_ref[q])
        for part in range(4):
            s_ref[0, :, part * S5_NS + q * LANES:part * S5_NS + (q + 1) * LANES] = (
                s[:, part * LANES:(part + 1) * LANES])


def _s5_states(u4, ws, bsz, seq, cb):
    lb = cb * S5_Q
    nblk = seq // lb
    nchunk = seq // S5_Q
    out = pl.BlockSpec((1, cb, 4 * S5_NS), lambda b, i: (b, i, 0))
    return pl.pallas_call(
        functools.partial(_s5_states_kernel, cb=cb),
        grid=(bsz, nblk),
        in_specs=[pl.BlockSpec((D_S5 // LANES, lb, LANES), lambda b, i: (0, b * nblk + i, 0)),
                  _const_spec(ws.shape)],
        out_specs=[out, out],
        out_shape=[jax.ShapeDtypeStruct((bsz, nchunk, S5_G * S5_Q * S5_H), BF16),
                   jax.ShapeDtypeStruct((bsz, nchunk, 4 * S5_NS), F32)],
        compiler_params=_cparams("parallel", "parallel"),
        name="s5_states",
    )(u4, ws)


def _s5_scan_kernel(sf_ref, sb_ref, a16_ref, hf_ref, hb_ref, hin_scr, h_scr, *, cb):
    i = pl.program_id(1)

    @pl.when(i == 0)
    def _():
        h_scr[...] = jnp.zeros_like(h_scr)

    for d, (s_ref, o_ref) in enumerate(((sf_ref, hf_ref), (sb_ref, hb_ref))):
        ar = a16_ref[d, 0:1, :]
        ai = a16_ref[d, 1:2, :]

        def step(r, carry, d=d, s_ref=s_ref, ar=ar, ai=ai):
            hr, hi = carry
            rr = r if d == 0 else cb - 1 - r
            hin_scr[pl.ds(rr, 1), 0:S5_NS] = hr
            hin_scr[pl.ds(rr, 1), S5_NS:2 * S5_NS] = hi
            sr = s_ref[0, pl.ds(rr, 1), 0:S5_NS]
            si = s_ref[0, pl.ds(rr, 1), S5_NS:2 * S5_NS]
            return ar * hr - ai * hi + sr, ar * hi + ai * hr + si

        hr, hi = lax.fori_loop(0, cb, step, (h_scr[d, 0:1, 0:S5_NS], h_scr[d, 0:1, S5_NS:2 * S5_NS]))
        h_scr[d, 0:1, 0:S5_NS] = hr
        h_scr[d, 0:1, S5_NS:2 * S5_NS] = hi
        o_ref[0] = hin_scr[...].astype(BF16)


def _s5_scan(s, a16, bsz, seq, cb):
    nchunk = seq // S5_Q
    nblk = nchunk // cb
    w = 2 * S5_NS
    return pl.pallas_call(
        functools.partial(_s5_scan_kernel, cb=cb),
        grid=(bsz, nblk),
        in_specs=[pl.BlockSpec((1, cb, w), lambda b, i: (b, i, 0)),
                  pl.BlockSpec((1, cb, w), lambda b, i: (b, nblk - 1 - i, 1)),
                  _const_spec(a16.shape)],
        out_specs=[pl.BlockSpec((1, cb, w), lambda b, i: (b, i, 0)),
                   pl.BlockSpec((1, cb, w), lambda b, i: (b, nblk - 1 - i, 0))],
        out_shape=[jax.ShapeDtypeStruct((bsz, nchunk, w), BF16)] * 2,
        scratch_shapes=[pltpu.VMEM((cb, w), F32),
                        pltpu.VMEM((2, SUBLANES, w), F32)],
        compiler_params=_cparams("arbitrary", "arbitrary"),
        name="s5_scan",
    )(s, s, a16)


def _s5_main_kernel(u_ref, uf_ref, hf_ref, hb_ref, ga_ref, t_ref, o_ref, d_ref, wg_ref, wa_ref,
                    ma_ref, y_scr, *, cb):
    yg = []
    nq = S5_Q * S5_H
    for q in range(S5_G // 2):
        sl_r = slice(q * LANES, (q + 1) * LANES)
        sl_i = slice(S5_NS + q * LANES, S5_NS + (q + 1) * LANES)
        hp = jnp.concatenate([hf_ref[0, :, sl_r], hf_ref[0, :, sl_i],
                              hb_ref[0, :, sl_r], hb_ref[0, :, sl_i]], axis=1)
        yh = _mm(hp, o_ref[q])
        for a in range(2):
            g = 2 * q + a
            yg.append(_mm(uf_ref[0, :, g * nq:(g + 1) * nq], t_ref[g]) + yh[:, a * nq:(a + 1) * nq])
    gps = LANES // S5_H
    for t in range(S5_Q):
        for k in range(D_S5 // LANES):
            z = jnp.concatenate(
                [yg[k * gps + m][:, S5_H * t:S5_H * (t + 1)] for m in range(gps)], axis=1)
            y_scr[k, pl.ds(t, cb, stride=S5_Q), :] = z
    u = jnp.concatenate([u_ref[k] for k in range(D_S5 // LANES)], axis=1)
    y = jnp.concatenate([y_scr[k] for k in range(D_S5 // LANES)], axis=1) + d_ref[...] * u
    c0 = 2.0 * math.sqrt(2.0 / math.pi)
    y = y * _sigmoid(c0 * (y + 0.044715 * (y * y * y)))
    y = y * _sigmoid(_mm(y.astype(BF16), wg_ref[...]))
    ma_ref[0] = _sigmoid(ga_ref[0]) * _mm(y.astype(BF16), wa_ref[...])


def _s5_main(u4, uf, hf, hb, proj3, tmat, omat, s5_d, w_glu, w_a, bsz, seq, cb):
    lb = cb * S5_Q
    nblk = seq // lb
    chunk_blk = lambda w: pl.BlockSpec((1, cb, w), lambda b, i: (b, i, 0))
    return pl.pallas_call(
        functools.partial(_s5_main_kernel, cb=cb),
        grid=(bsz, nblk),
        in_specs=[pl.BlockSpec((D_S5 // LANES, lb, LANES), lambda b, i: (0, b * nblk + i, 0)),
                  chunk_blk(S5_G * S5_Q * S5_H), chunk_blk(2 * S5_NS), chunk_blk(2 * S5_NS),
                  pl.BlockSpec((1, lb, D_MODEL), lambda b, i: (b, i, COL_GA // D_MODEL)),
                  _const_spec(tmat.shape), _const_spec(omat.shape), _const_spec((1, D_S5)),
                  _const_spec((D_S5, D_S5)), _const_spec((D_S5, D_MODEL))],
        out_specs=pl.BlockSpec((1, lb, D_MODEL), lambda b, i: (b, i, 0)),
        out_shape=jax.ShapeDtypeStruct((bsz, seq, D_MODEL), F32),
        scratch_shapes=[pltpu.VMEM((D_S5 // LANES, lb, LANES), F32)],
        compiler_params=_cparams("parallel", "parallel"),
        name="s5_main",
    )(u4, uf, hf, hb, proj3, tmat, omat, s5_d, w_glu, w_a)


def _softplus(x):
    return jnp.maximum(x, 0.0) + jnp.log(1.0 + jnp.exp(-jnp.abs(x)))


def _ssd_decays(dt_raw, dtb_ref, alog_ref):
    dtv = _softplus(dt_raw + dtb_ref[...])
    dta = dtv * (-jnp.exp(alog_ref[...]))
    row = lax.broadcasted_iota(jnp.int32, (SSD_Q, SSD_Q), 0)
    col = lax.broadcasted_iota(jnp.int32, (SSD_Q, SSD_Q), 1)
    inc = dta
    sh = 1
    while sh < SSD_Q:
        inc = inc + jnp.where(row >= sh, pltpu.roll(inc, sh, 0), 0.0)
        sh *= 2
    return dtv, dta, inc, row, col


def _head_row(v_row, sel):
    return _mm_exact_rhs(jnp.broadcast_to(v_row, (SUBLANES, LANES)), sel)[0:1, :]


def _ssd_pre_kernel(cv_ref, dt_ref, dtb_ref, alog_ref, eb_ref, hb_ref, h_scr, *, nch):
    i = pl.program_id(1)
    gw = SSD_HPG * SSD_P

    @pl.when(i == 0)
    def _():
        h_scr[...] = jnp.zeros_like(h_scr)

    def chunk(k, carry):
        ci = nch - 1 - k
        r0 = pl.multiple_of(ci * SSD_Q, SSD_Q)
        xs = cv_ref[0, pl.ds(r0, SSD_Q), 0:D_SSD].astype(F32)
        bm = cv_ref[0, pl.ds(r0, SSD_Q), D_SSD:D_SSD + SSD_G * SSD_N]
        dtv, dta, inc, _, _ = _ssd_decays(dt_ref[0, pl.ds(r0, SSD_Q), :], dtb_ref, alog_ref)
        exc = inc - dta
        wb = _mm((jnp.exp(exc) * dtv).astype(BF16), eb_ref[...])
        decb = _head_row(jnp.exp(inc[SSD_Q - 1:SSD_Q, :]), eb_ref[...])
        xw = (xs * wb).astype(BF16)
        for g in range(SSD_G):
            hg = h_scr[g]
            hb_ref[0, ci, g] = hg.astype(BF16)
            bg = bm[:, g * SSD_N:(g + 1) * SSD_N]
            sg = lax.dot_general(bg, xw[:, g * gw:(g + 1) * gw], (((0,), (0,)), ((), ())),
                                 preferred_element_type=F32)
            h_scr[g] = hg * decb[:, g * gw:(g + 1) * gw] + sg
        return carry

    lax.fori_loop(0, nch, chunk, 0)


def _ssd_pre(cv3, proj3, dtb, alog, eb, bsz, seq, nch):
    tb = nch * SSD_Q
    nblk = seq // tb
    gw = SSD_HPG * SSD_P
    return pl.pallas_call(
        functools.partial(_ssd_pre_kernel, nch=nch),
        grid=(bsz, nblk),
        in_specs=[pl.BlockSpec((1, tb, SSD_CH), lambda b, i: (b, nblk - 1 - i, 0)),
                  pl.BlockSpec((1, tb, LANES), lambda b, i: (b, nblk - 1 - i, COL_DT // LANES)),
                  _const_spec(dtb.shape), _const_spec(alog.shape), _const_spec(eb.shape)],
        out_specs=pl.BlockSpec((1, nch, SSD_G, SSD_N, gw), lambda b, i: (b, nblk - 1 - i, 0, 0, 0)),
        out_shape=jax.ShapeDtypeStruct((bsz, seq // SSD_Q, SSD_G, SSD_N, gw), BF16),
        scratch_shapes=[pltpu.VMEM((SSD_G, SSD_N, gw), F32)],
        compiler_params=_cparams("arbitrary", "arbitrary"),
        name="ssd_pre",
    )(cv3, proj3, dtb, alog, eb)


def _ssd_main_kernel(cv_ref, dt_ref, z_ref, gb_ref, x_ref, ma_ref, hb_ref,
                     dtb_ref, alog_ref, alogc_ref, dfull_ref, nw_ref, ef_ref, eb_ref,
                     wb_ref, wo_ref, out_ref, hf_scr, y_scr, *, nch):
    i = pl.program_id(1)
    gw = SSD_HPG * SSD_P

    @pl.when(i == 0)
    def _():
        hf_scr[...] = jnp.zeros_like(hf_scr)

    def chunk(ci, carry):
        r0 = pl.multiple_of(ci * SSD_Q, SSD_Q)
        xb = cv_ref[0, pl.ds(r0, SSD_Q), 0:D_SSD]
        bm = cv_ref[0, pl.ds(r0, SSD_Q), D_SSD:D_SSD + SSD_G * SSD_N]
        cm = cv_ref[0, pl.ds(r0, SSD_Q), D_SSD + SSD_G * SSD_N:SSD_CH]
        dtv, dta, inc, row, col = _ssd_decays(dt_ref[0, pl.ds(r0, SSD_Q), :], dtb_ref, alog_ref)
        exc = inc - dta
        tot = inc[SSD_Q - 1:SSD_Q, :]
        dt_t = dtv.T
        dta_t = dt_t * (-jnp.exp(alogc_ref[...]))
        inc_t = inc.T
        ldt_t = jnp.log(dt_t)
        rowf_t = inc_t - ldt_t
        rowb_t = (inc_t - dta_t) + ldt_t
        diag_t = jnp.log(dt_t[0:SSD_HEADS] + dt_t[SSD_HEADS:2 * SSD_HEADS])
        ef_full = _mm(jnp.exp(inc).astype(BF16), ef_ref[...])
        wf_full = _mm((jnp.exp(tot - inc) * dtv).astype(BF16), ef_ref[...])
        eb_full = _mm(jnp.exp(tot - exc).astype(BF16), eb_ref[...])
        decf = _head_row(jnp.exp(tot), ef_ref[...])
        xwf = (xb.astype(F32) * wf_full).astype(BF16)
        lower = row > col
        upper = row < col
        lane_head = lax.broadcasted_iota(jnp.int32, (SSD_Q, gw), 1) // SSD_P
        for g in range(SSD_G):
            gs = slice(g * gw, (g + 1) * gw)
            bg = bm[:, g * SSD_N:(g + 1) * SSD_N]
            cg = cm[:, g * SSD_N:(g + 1) * SSD_N]
            cbm = lax.dot_general(cg, bg, (((1,), (1,)), ((), ())), preferred_element_type=F32)
            hfg = hf_scr[g]
            yg = (ef_full[:, gs] * _mm(cg, hfg.astype(BF16))
                  + eb_full[:, gs] * _mm(cg, hb_ref[0, ci, g]))
            xg = xb[:, gs]
            ms, xm = [], []
            for jj in range(SSD_HPG):
                j = g * SSD_HPG + jj
                jb = SSD_HEADS + j
                seg = jnp.where(lower, inc[:, j:j + 1] - rowf_t[j:j + 1, :],
                                jnp.where(upper, rowb_t[jb:jb + 1, :] - exc[:, jb:jb + 1],
                                          diag_t[j:j + 1, :]))
                ms.append((cbm * jnp.exp(seg)).astype(BF16))
                xm.append(jnp.where(lane_head == jj, xg, jnp.zeros_like(xg)))
            for jj in range(0, SSD_HPG, 2):
                yg = yg + _mm(jnp.concatenate(ms[jj:jj + 2], axis=1),
                              jnp.concatenate(xm[jj:jj + 2], axis=0))
            y_scr[pl.ds(r0, SSD_Q), gs] = yg
            sg = lax.dot_general(bg, xwf[:, gs], (((0,), (0,)), ((), ())),
                                 preferred_element_type=F32)
            hf_scr[g] = hfg * decf[:, gs] + sg
        return carry

    lax.fori_loop(0, nch, chunk, 0)

    y = y_scr[...] + dfull_ref[...] * cv_ref[0, :, 0:D_SSD].astype(F32)
    y = _rms(y * _silu(z_ref[0]), nw_ref[...])
    merged = ma_ref[0] + _sigmoid(gb_ref[0]) * _mm(y.astype(BF16), wb_ref[...])
    out_ref[0] = x_ref[0] + _mm(merged.astype(BF16), wo_ref[...])


def _ssd_main(cv3, proj3, x, ma, hb, dtb, alog, alogc, dfull, nw, ef, eb, w_b, w_o, bsz, seq, nch):
    tb = nch * SSD_Q
    nblk = seq // tb
    gw = SSD_HPG * SSD_P
    tok = pl.BlockSpec((1, tb, D_MODEL), lambda b, i: (b, i, 0))
    return pl.pallas_call(
        functools.partial(_ssd_main_kernel, nch=nch),
        grid=(bsz, nblk),
        in_specs=[pl.BlockSpec((1, tb, SSD_CH), lambda b, i: (b, i, 0)),
                  pl.BlockSpec((1, tb, LANES), lambda b, i: (b, i, COL_DT // LANES)),
                  pl.BlockSpec((1, tb, D_SSD), lambda b, i: (b, i, COL_Z // D_SSD)),
                  pl.BlockSpec((1, tb, D_MODEL), lambda b, i: (b, i, COL_GB // D_MODEL)),
                  tok, tok,
                  pl.BlockSpec((1, nch, SSD_G, SSD_N, gw), lambda b, i: (b, i, 0, 0, 0)),
                  _const_spec(dtb.shape), _const_spec(alog.shape), _const_spec(alogc.shape),
                  _const_spec(dfull.shape), _const_spec(nw.shape),
                  _const_spec(ef.shape), _const_spec(eb.shape), _const_spec(w_b.shape),
                  _const_spec(w_o.shape)],
        out_specs=tok,
        out_shape=jax.ShapeDtypeStruct((bsz, seq, D_MODEL), F32),
        scratch_shapes=[pltpu.VMEM((SSD_G, SSD_N, gw), F32),
                        pltpu.VMEM((tb, D_SSD), F32)],
        compiler_params=_cparams("arbitrary", "arbitrary"),
        name="ssd_main",
    )(cv3, proj3, proj3, proj3, x, ma, hb, dtb, alog, alogc, dfull, nw, ef, eb, w_b, w_o)


def _memkv_kernel(m_ref, g_ref, wk_ref, wv_ref, k_ref, v_ref):
    n = _rms(m_ref[0], g_ref[...]).astype(BF16)
    k_ref[0] = _mm(n, wk_ref[...]).astype(BF16)
    v_ref[0] = _mm(n, wv_ref[...]).astype(BF16)


def _memkv(mem, g, wk, wv):
    bsz, m, _ = mem.shape
    spec = pl.BlockSpec((1, m, D_MODEL), lambda b: (b, 0, 0))
    return pl.pallas_call(
        _memkv_kernel,
        grid=(bsz,),
        in_specs=[spec, _const_spec((1, D_MODEL)), _const_spec(wk.shape), _const_spec(wv.shape)],
        out_specs=[spec, spec],
        out_shape=[jax.ShapeDtypeStruct(mem.shape, BF16)] * 2,
        compiler_params=_cparams("parallel"),
        name="memkv",
    )(mem, g, wk, wv)


def _xattn_kernel(x_ref, k_ref, v_ref, g_ref, wq_ref, wo_ref, o_ref):
    x = x_ref[0]
    q = _mm(_rms(x, g_ref[...]).astype(BF16), wq_ref[...])
    scale = XA_HD ** -0.5
    heads = []
    for h in range(XA_HEADS):
        sl = slice(h * XA_HD, (h + 1) * XA_HD)
        s = lax.dot_general(q[:, sl].astype(BF16), k_ref[0, :, sl], (((1,), (1,)), ((), ())),
                            preferred_element_type=F32) * scale
        p = jnp.exp(s - jnp.max(s, axis=-1, keepdims=True))
        p = p / jnp.sum(p, axis=-1, keepdims=True)
        heads.append(_mm(p.astype(BF16), v_ref[0, :, sl]))
    o = jnp.concatenate(heads, axis=1).astype(BF16)
    o_ref[0] = x + _mm(o, wo_ref[...])


def _xattn(x, k, v, g, wq, wo, tm=512):
    bsz, seq, _ = x.shape
    m = k.shape[1]
    tok = pl.BlockSpec((1, tm, D_MODEL), lambda b, i: (b, i, 0))
    kv = pl.BlockSpec((1, m, D_MODEL), lambda b, i: (b, 0, 0))
    return pl.pallas_call(
        _xattn_kernel,
        grid=(bsz, seq // tm),
        in_specs=[tok, kv, kv, _const_spec((1, D_MODEL)), _const_spec(wq.shape), _const_spec(wo.shape)],
        out_specs=tok,
        out_shape=jax.ShapeDtypeStruct(x.shape, F32),
        compiler_params=_cparams("parallel", "parallel"),
        name="xattn",
    )(x, k, v, g, wq, wo)


def _mlp_kernel(x_ref, g_ref, wu_ref, wd_ref, gf_ref, o_ref, *, final):
    x = x_ref[...]
    n = _rms(x, g_ref[...]).astype(BF16)
    acc = x
    fc = D_MODEL
    for c in range(D_FF // fc):
        h = jnp.maximum(_mm(n, wu_ref[:, c * fc:(c + 1) * fc]), 0.0)
        acc = acc + _mm((h * h).astype(BF16), wd_ref[c * fc:(c + 1) * fc, :])
    o_ref[...] = _rms(acc, gf_ref[...]) if final else acc


def _mlp(x2d, g, wu, wd, gf, final, tm=512):
    t = x2d.shape[0]
    tok = pl.BlockSpec((tm, D_MODEL), lambda i: (i, 0))
    return pl.pallas_call(
        functools.partial(_mlp_kernel, final=final),
        grid=(t // tm,),
        in_specs=[tok, _const_spec((1, D_MODEL)), _const_spec(wu.shape), _const_spec(wd.shape),
                  _const_spec((1, D_MODEL))],
        out_specs=tok,
        out_shape=jax.ShapeDtypeStruct(x2d.shape, F32),
        compiler_params=_cparams("parallel"),
        name="mlp_final" if final else "mlp",
    )(x2d, g, wu, wd, gf)


def _row(v, width=None):
    v = v.astype(F32).reshape(1, -1)
    if width is not None and v.shape[1] < width:
        v = jnp.pad(v, ((0, 0), (0, width - v.shape[1])))
    return v


def _head_selectors():
    lane = jnp.arange(LANES)[:, None]
    head = (jnp.arange(D_SSD) // SSD_P)[None, :]
    return (lane == head).astype(BF16), (lane == head + SSD_HEADS).astype(BF16)


def _pack_layer(p, i):
    w_in = p['w_in'][i]
    o_u, o_z, o_x, o_dt = D_S5, D_S5 + D_SSD, D_S5 + D_SSD + SSD_CH, D_S5 + D_SSD + SSD_CH + 2 * SSD_HEADS
    w_dt = jnp.pad(w_in[:, o_x:o_dt], ((0, 0), (0, LANES - 2 * SSD_HEADS)))
    w_packed = jnp.concatenate(
        [w_in[:, o_u:o_z], w_in[:, o_dt:o_dt + D_MODEL], w_in[:, o_dt + D_MODEL:], w_dt,
         w_in[:, o_z:o_x], w_in[:, :o_u]], axis=1).astype(BF16)
    alog = _row(p['ssd_a_log'][i], LANES)
    tmat, ws, a16, omat = _s5_tables(p['s5_lam_re'][i], p['s5_lam_im'][i], p['s5_log_dt'][i],
                                     p['s5_b_re'][i], p['s5_b_im'][i], p['s5_c_re'][i], p['s5_c_im'][i])
    cw = jnp.pad(p['ssd_conv_w'][i].astype(F32), ((0, SUBLANES - SSD_CONV), (0, 0)))
    return dict(
        g_mix=_row(p['norm_mix'][i]), w_in=w_packed,
        tmat=tmat, ws=ws, a16=a16, omat=omat, s5_d=_row(p['s5_d'][i]),
        w_glu=p['s5_w_glu'][i].astype(BF16), w_a=p['w_branch_a'][i].astype(BF16),
        cw=cw, cbias=_row(p['ssd_conv_b'][i]), dtb=_row(p['ssd_dt_bias'][i], LANES),
        alog=alog, alogc=jnp.broadcast_to(alog.reshape(LANES, 1), (LANES, LANES)), dfull=_row(jnp.repeat(p['ssd_d'][i], SSD_P)),
        nw=_row(p['ssd_norm'][i]), w_b=p['w_branch_b'][i].astype(BF16), w_o=p['w_out'][i].astype(BF16),
        g_xa=_row(p['norm_xattn'][i]), g_mem=_row(p['norm_mem'][i]),
        wq=p['w_q'][i].astype(BF16), wk=p['w_k'][i].astype(BF16), wv=p['w_v'][i].astype(BF16),
        wo=p['w_o'][i].astype(BF16),
        g_mlp=_row(p['norm_mlp'][i]), wu=p['w_up'][i].astype(BF16), wd=p['w_down'][i].astype(BF16))


def _trunk(x, mem, layers, g_final, s5_cb, ssd_nch):
    bsz, seq, _ = x.shape
    t = bsz * seq
    ef, eb = _head_selectors()
    for li, lp in enumerate(layers):
        proj, cv, u4 = _inproj(x.reshape(t, D_MODEL), lp['g_mix'], lp['w_in'], lp['cw'], lp['cbias'], seq)
        proj3 = proj.reshape(bsz, seq, PROJ_COLS)
        cv3 = cv.reshape(bsz, seq, SSD_CH)
        uf, s5s = _s5_states(u4, lp['ws'], bsz, seq, s5_cb)
        hf, hb5 = _s5_scan(s5s, lp['a16'], bsz, seq, s5_cb)
        ma = _s5_main(u4, uf, hf, hb5, proj3, lp['tmat'], lp['omat'], lp['s5_d'], lp['w_glu'],
                      lp['w_a'], bsz, seq, s5_cb)
        hb = _ssd_pre(cv3, proj3, lp['dtb'], lp['alog'], eb, bsz, seq, ssd_nch)
        x = _ssd_main(cv3, proj3, x, ma, hb, lp['dtb'], lp['alog'], lp['alogc'], lp['dfull'],
                      lp['nw'], ef, eb, lp['w_b'], lp['w_o'], bsz, seq, ssd_nch)
        k, v = _memkv(mem, lp['g_mem'], lp['wk'], lp['wv'])
        x = _xattn(x, k, v, lp['g_xa'], lp['wq'], lp['wo'])
        x = _mlp(x.reshape(t, D_MODEL), lp['g_mlp'], lp['wu'], lp['wd'], g_final,
                 final=li == len(layers) - 1).reshape(bsz, seq, D_MODEL)
    return x


def kernel(x_prompt, x_sample, mem_prompt, mem_sample, norm_mix, w_in, s5_lam_re, s5_lam_im, s5_log_dt, s5_b_re, s5_b_im, s5_c_re, s5_c_im, s5_d, s5_w_glu, ssd_conv_w, ssd_conv_b, ssd_a_log, ssd_dt_bias, ssd_d, ssd_norm, w_branch_a, w_branch_b, w_out, norm_xattn, norm_mem, w_q, w_k, w_v, w_o, norm_mlp, w_up, w_down, norm_final):
    p = dict(norm_mix=norm_mix, w_in=w_in,
             s5_lam_re=s5_lam_re, s5_lam_im=s5_lam_im, s5_log_dt=s5_log_dt,
             s5_b_re=s5_b_re, s5_b_im=s5_b_im, s5_c_re=s5_c_re, s5_c_im=s5_c_im,
             s5_d=s5_d, s5_w_glu=s5_w_glu,
             ssd_conv_w=ssd_conv_w, ssd_conv_b=ssd_conv_b, ssd_a_log=ssd_a_log,
             ssd_dt_bias=ssd_dt_bias, ssd_d=ssd_d, ssd_norm=ssd_norm,
             w_branch_a=w_branch_a, w_branch_b=w_branch_b, w_out=w_out,
             norm_xattn=norm_xattn, norm_mem=norm_mem, w_q=w_q, w_k=w_k, w_v=w_v, w_o=w_o,
             norm_mlp=norm_mlp, w_up=w_up, w_down=w_down)
    layers = [_pack_layer(p, i) for i in range(norm_mix.shape[0])]
    g_final = _row(norm_final)
    y_prompt = _trunk(x_prompt, mem_prompt, layers, g_final, s5_cb=64, ssd_nch=4)
    y_sample = _trunk(x_sample, mem_sample, layers, g_final, s5_cb=64, ssd_nch=4)
    return (y_prompt, y_sample)
```
